```python
import math
import jax, jax.numpy as jnp
from jax import lax
import numpy as np

D_MODEL = 1024
BATCH = 2
SEQ = 8192
DEPTH = 4
DEC_BATCH = 128
DEC_SEQ = 8
PAST_LEN = 8192
PAGE_SIZE = 128

N_META = 16
N_MIXERS = 2
N_POOL_LAYERS = (DEPTH + 1) // 2
N_MLA_LAYERS = DEPTH // 2
POOL_WINDOWS = (2, 4, 8, 16)
N_POOL_GROUPS = len(POOL_WINDOWS)
POOL_GROUP = D_MODEL // N_POOL_GROUPS
POOL_STATE = max(POOL_WINDOWS) - 1
N_HEADS = 8
QK_NOPE = 128
QK_ROPE = 64
V_HEAD = 128
Q_LORA = D_MODEL // 2
KV_LORA = D_MODEL // 4
KV_ROW = KV_LORA + QK_ROPE
ROPE_THETA = 10000.0
SOFTMAX_SCALE = (QK_NOPE + QK_ROPE) ** -0.5
Q_BLOCK = 128
D_FF = -(-(8 * D_MODEL) // (3 * 256)) * 256
ALPHA = (2 * DEPTH) ** 0.25
BETA = (8 * DEPTH) ** -0.25
LN_EPS = 1e-5
RMS_EPS = 1e-6
NEG_INF = -1e30

kernel_name = "hybrid_pool_mla_deepnorm_step"


def layer_norm(x, g, b):
    xf = x.astype(jnp.float32)
    mu = jnp.mean(xf, axis=-1, keepdims=True)
    var = jnp.mean(jnp.square(xf - mu), axis=-1, keepdims=True)
    y = (xf - mu) * lax.rsqrt(var + LN_EPS) * g.astype(jnp.float32) + b.astype(jnp.float32)
    return y.astype(x.dtype)


def rms_norm(x, g):
    xf = x.astype(jnp.float32)
    y = xf * lax.rsqrt(jnp.mean(jnp.square(xf), axis=-1, keepdims=True) + RMS_EPS)
    return (y * g.astype(jnp.float32)).astype(x.dtype)


def rope_cos_sin(pos):
    freqs = ROPE_THETA ** (-jnp.arange(0, QK_ROPE, 2, dtype=jnp.float32) / QK_ROPE)
    ang = pos.astype(jnp.float32)[:, None] * freqs[None, :]
    return jnp.cos(ang), jnp.sin(ang)


def apply_rope(x, cos, sin):
    half = QK_ROPE // 2
    xf = x.astype(jnp.float32)
    x1, x2 = xf[..., :half], xf[..., half:]
    return jnp.concatenate([x1 * cos - x2 * sin, x2 * cos + x1 * sin], axis=-1).astype(x.dtype)


def pool_mixer(ext, n_prev, w, scale):
    n, tot, d = ext.shape
    t = tot - n_prev
    cs = jnp.cumsum(ext.astype(jnp.float32), axis=1)
    cs = jnp.pad(cs, ((0, 0), (1, 0), (0, 0)))
    end = n_prev + jnp.arange(t) + 1
    cur = ext[:, n_prev:].astype(jnp.float32)
    parts = []
    for g, win in enumerate(POOL_WINDOWS):
        lo, hi = g * POOL_GROUP, (g + 1) * POOL_GROUP
        start = jnp.maximum(end - win, 0)
        cnt = (end - start).astype(jnp.float32)[:, None]
        mean = (cs[:, end, lo:hi] - cs[:, start, lo:hi]) / cnt
        parts.append(mean - cur[..., lo:hi])
    pooled = jnp.stack(parts, axis=2).astype(ext.dtype)
    out = jnp.einsum('ntgc,gcd->ntgd', pooled, w).reshape(n, t, d)
    return out * scale


def mla_project(h, pos, w_dq, g_q, w_uq, w_dkv, g_kv, w_uk):
    cos, sin = rope_cos_sin(pos)
    c_q = rms_norm(h @ w_dq, g_q)
    q = jnp.einsum('ntq,qhe->nthe', c_q, w_uq)
    q_nope, q_pe = q[..., :QK_NOPE], q[..., QK_NOPE:]
    q_pe = apply_rope(q_pe, cos[:, None, :], sin[:, None, :])
    q_lat = jnp.einsum('nthe,rhe->nthr', q_nope, w_uk)
    kv = h @ w_dkv
    c_kv = rms_norm(kv[..., :KV_LORA], g_kv)
    k_pe = apply_rope(kv[..., KV_LORA:], cos, sin)
    return q_lat, q_pe, jnp.concatenate([c_kv, k_pe], axis=-1)


def latent_attention(q_lat, q_pe, rows, q_pos, k_pos):
    c_kv, k_pe = rows[:, :KV_LORA], rows[:, KV_LORA:]
    s = (jnp.einsum('qhr,kr->hqk', q_lat, c_kv, preferred_element_type=jnp.float32)
         + jnp.einsum('qhp,kp->hqk', q_pe, k_pe, preferred_element_type=jnp.float32)) * SOFTMAX_SCALE
    s = jnp.where(k_pos[None, None, :] <= q_pos[None, :, None], s, NEG_INF)
    p = jax.nn.softmax(s, axis=-1)
    o = jnp.einsum('hqk,kr->qhr', p.astype(rows.dtype), c_kv, preferred_element_type=jnp.float32)
    return o.astype(q_lat.dtype)


def mla_out(o_lat, w_uv, w_o):
    n, t = o_lat.shape[:2]
    o = jnp.einsum('nthr,rhv->nthv', o_lat, w_uv)
    return o.reshape(n, t, N_HEADS * V_HEAD) @ w_o


def mla_prompt_attention(q_lat, q_pe, rows):
    n, l = q_lat.shape[:2]
    nb = -(-l // Q_BLOCK)
    lp = nb * Q_BLOCK
    pad = ((0, 0), (0, lp - l), (0, 0), (0, 0))
    qlb = jnp.pad(q_lat, pad).reshape(n, nb, Q_BLOCK, N_HEADS, KV_LORA).transpose(1, 0, 2, 3, 4)
    qpb = jnp.pad(q_pe, pad).reshape(n, nb, Q_BLOCK, N_HEADS, QK_ROPE).transpose(1, 0, 2, 3, 4)
    k_pos = jnp.arange(l)

    def block(args):
        ql, qp, bi = args
        q_pos = bi * Q_BLOCK + jnp.arange(Q_BLOCK)
        return jax.vmap(latent_attention, in_axes=(0, 0, 0, None, None))(ql, qp, rows, q_pos, k_pos)

    o = lax.map(block, (qlb, qpb, jnp.arange(nb)))
    return o.transpose(1, 0, 2, 3, 4).reshape(n, lp, N_HEADS, KV_LORA)[:, :l]


def mla_sample_attention(q_lat, q_pe, new_rows, cache, page_table):
    t = q_lat.shape[1]
    q_pos = PAST_LEN + jnp.arange(t)
    k_pos = jnp.arange(PAST_LEN + t)

    def one_seq(args):
        ql, qp, nr, pages = args
        past = cache[pages].reshape(-1, KV_ROW)
        keys = jnp.concatenate([past, nr.astype(past.dtype)], axis=0)
        return latent_attention(ql, qp, keys, q_pos, k_pos)

    return lax.map(one_seq, (q_lat, q_pe, new_rows, page_table))


def swiglu(h, w_gate, w_up, w_down):
    return (jax.nn.silu(h @ w_gate) * (h @ w_up)) @ w_down


def setup_inputs(seed: int = 0) -> dict:
    key = jax.random.key(seed)
    ks = jax.random.split(key, 24)
    f32 = jnp.float32
    nrm = lambda k, shape, s: jax.random.normal(k, shape, f32) * s
    n_pages = PAST_LEN // PAGE_SIZE
    n_used = DEC_BATCH * n_pages
    n_phys = n_used + (n_used + 3) // 4
    page_table = jax.random.permutation(ks[0], n_phys)[:n_used].reshape(DEC_BATCH, n_pages).astype(jnp.int32)
    return {
        "x_prompt": nrm(ks[1], (BATCH, SEQ, D_MODEL), 1.0),
        "x_sample": nrm(ks[2], (DEC_BATCH, DEC_SEQ, D_MODEL), 1.0),
        "state_pool_l0": nrm(ks[3], (DEC_BATCH, POOL_STATE, D_MODEL), 1.0),
        "cache_mla_l1": nrm(ks[4], (n_phys, PAGE_SIZE, KV_ROW), 1.0),
        "state_pool_l2": nrm(ks[5], (DEC_BATCH, POOL_STATE, D_MODEL), 1.0),
        "cache_mla_l3": nrm(ks[6], (n_phys, PAGE_SIZE, KV_ROW), 1.0),
        "page_table": page_table,
        "meta_tokens": nrm(ks[7], (N_META, D_MODEL), 1.0),
        "pool_w": nrm(ks[8], (N_POOL_LAYERS, N_POOL_GROUPS, POOL_GROUP, POOL_GROUP), BETA * POOL_GROUP ** -0.5),
        "pool_scale": 1.0 + nrm(ks[9], (N_POOL_LAYERS, D_MODEL), 0.1),
        "mla_w_dq": nrm(ks[10], (N_MLA_LAYERS, D_MODEL, Q_LORA), D_MODEL ** -0.5),
        "mla_g_q": 1.0 + nrm(ks[11], (N_MLA_LAYERS, Q_LORA), 0.02),
        "mla_w_uq": nrm(ks[12], (N_MLA_LAYERS, Q_LORA, N_HEADS, QK_NOPE + QK_ROPE), Q_LORA ** -0.5),
        "mla_w_dkv": nrm(ks[13], (N_MLA_LAYERS, D_MODEL, KV_ROW), D_MODEL ** -0.5),
        "mla_g_kv": 1.0 + nrm(ks[14], (N_MLA_LAYERS, KV_LORA), 0.02),
        "mla_w_uk": nrm(ks[15], (N_MLA_LAYERS, KV_LORA, N_HEADS, QK_NOPE), KV_LORA ** -0.5),
        "mla_w_uv": nrm(ks[16], (N_MLA_LAYERS, KV_LORA, N_HEADS, V_HEAD), BETA * KV_LORA ** -0.5),
        "mla_w_o": nrm(ks[17], (N_MLA_LAYERS, N_HEADS * V_HEAD, D_MODEL), BETA * (N_HEADS * V_HEAD) ** -0.5),
        "ffn_w_gate": nrm(ks[18], (DEPTH, D_MODEL, D_FF), D_MODEL ** -0.5),
        "ffn_w_up": nrm(ks[19], (DEPTH, D_MODEL, D_FF), D_MODEL ** -0.5),
        "ffn_w_down": nrm(ks[20], (DEPTH, D_FF, D_MODEL), BETA * D_FF ** -0.5),
        "ln_g": 1.0 + nrm(ks[21], (DEPTH, 2, D_MODEL), 0.02),
        "ln_b": nrm(ks[22], (DEPTH, 2, D_MODEL), 0.02),
    }


def reference(x_prompt, x_sample, state_pool_l0, cache_mla_l1, state_pool_l2, cache_mla_l3, page_table,
              meta_tokens, pool_w, pool_scale, mla_w_dq, mla_g_q, mla_w_uq, mla_w_dkv, mla_g_kv,
              mla_w_uk, mla_w_uv, mla_w_o, ffn_w_gate, ffn_w_up, ffn_w_down, ln_g, ln_b):
    b = x_prompt.shape[0]
    meta = jnp.broadcast_to(meta_tokens[None].astype(x_prompt.dtype), (b, N_META, D_MODEL))
    hp = jnp.concatenate([meta, x_prompt], axis=1)
    hs = x_sample
    pos_p = jnp.arange(hp.shape[1])
    pos_s = PAST_LEN + jnp.arange(hs.shape[1])
    pool_states = (state_pool_l0, state_pool_l2)
    mla_caches = (cache_mla_l1, cache_mla_l3)
    new_state = []
    for i in range(DEPTH):
        j = i // N_MIXERS
        if i % N_MIXERS == 0:
            ext_s = jnp.concatenate([pool_states[j].astype(hs.dtype), hs], axis=1)
            mix_p = pool_mixer(hp, 0, pool_w[j], pool_scale[j])
            mix_s = pool_mixer(ext_s, POOL_STATE, pool_w[j], pool_scale[j])
            new_state += [hp[:, -POOL_STATE:], ext_s[:, -POOL_STATE:]]
        else:
            wts = (mla_w_dq[j], mla_g_q[j], mla_w_uq[j], mla_w_dkv[j], mla_g_kv[j], mla_w_uk[j])
            ql_p, qp_p, rows_p = mla_project(hp, pos_p, *wts)
            ql_s, qp_s, rows_s = mla_project(hs, pos_s, *wts)
            o_p = mla_prompt_attention(ql_p, qp_p, rows_p)
            o_s = mla_sample_attention(ql_s, qp_s, rows_s, mla_caches[j], page_table)
            mix_p = mla_out(o_p, mla_w_uv[j], mla_w_o[j])
            mix_s = mla_out(o_s, mla_w_uv[j], mla_w_o[j])
            new_state += [rows_p, rows_s]
        hp = layer_norm(ALPHA * hp + mix_p, ln_g[i, 0], ln_b[i, 0])
        hs = layer_norm(ALPHA * hs + mix_s, ln_g[i, 0], ln_b[i, 0])
        hp = layer_norm(ALPHA * hp + swiglu(hp, ffn_w_gate[i], ffn_w_up[i], ffn_w_down[i]), ln_g[i, 1], ln_b[i, 1])
        hs = layer_norm(ALPHA * hs + swiglu(hs, ffn_w_gate[i], ffn_w_up[i], ffn_w_down[i]), ln_g[i, 1], ln_b[i, 1])
    y_prompt = hp[:, N_META:]
    y_sample = hs
    return (y_prompt, y_sample, *new_state)
```

```python
import functools
import math

import jax
import jax.numpy as jnp
from jax import lax
from jax.experimental import pallas as pl
from jax.experimental.pallas import tpu as pltpu

F32 = jnp.float32
BF16 = jnp.bfloat16

D_MODEL = 1024
BATCH = 2
SEQ = 8192
DEPTH = 4
DEC_BATCH = 128
DEC_SEQ = 8
PAST_LEN = 8192
PAGE_SIZE = 128
N_META = 16
POOL_WINDOWS = (2, 4, 8, 16)
POOL_GROUP = D_MODEL // len(POOL_WINDOWS)
POOL_STATE = max(POOL_WINDOWS) - 1
N_HEADS = 8
QK_NOPE = 128
QK_ROPE = 64
V_HEAD = 128
Q_LORA = D_MODEL // 2
KV_LORA = D_MODEL // 4
KV_ROW = KV_LORA + QK_ROPE
ROPE_THETA = 10000.0
SOFTMAX_SCALE = (QK_NOPE + QK_ROPE) ** -0.5
D_FF = -(-(8 * D_MODEL) // (3 * 256)) * 256
ALPHA = (2 * DEPTH) ** 0.25
LN_EPS = 1e-5
RMS_EPS = 1e-6
NEG_INF = -1e30

LANES = 128
T_REAL = SEQ + N_META
TP = -(-T_REAL // LANES) * LANES
ROPE_PAD = LANES
QK_CAT = KV_LORA + ROPE_PAD
EXP2_SCALE = SOFTMAX_SCALE * math.log2(math.e)

TM_PROMPT = 832
TM_SAMPLE = 512
FF_CHUNK = 256
ATT_BQ = 128
ATT_BK = 512
POOL_BS = 16
SA_PAGES = 16
SA_KEYS = SA_PAGES * PAGE_SIZE
SA_NEW_PAD = 16
VMEM_LIMIT = 56 * 1024 * 1024


def _params(*sem):
    return pltpu.CompilerParams(dimension_semantics=sem, vmem_limit_bytes=VMEM_LIMIT)


def _resident(shape):
    nd = len(shape)
    return pl.BlockSpec(shape, lambda *_: (0,) * nd, pipeline_mode=pl.Buffered(1))


def _layer_norm(y, g, b):
    mu = jnp.mean(y, axis=-1, keepdims=True)
    d = y - mu
    var = jnp.mean(d * d, axis=-1, keepdims=True)
    return d * lax.rsqrt(var + LN_EPS) * g + b


def _rms_norm(x, g):
    return x * lax.rsqrt(jnp.mean(x * x, axis=-1, keepdims=True) + RMS_EPS) * g


def _dot(a, b):
    return jnp.dot(a, b, preferred_element_type=F32)


def _dot_nt(a, b):
    return lax.dot_general(a, b, (((1,), (1,)), ((), ())), preferred_element_type=F32)


def _ffn_ln_kernel(x_ref, wg_ref, wu_ref, wd_ref, g_ref, b_ref, o_ref):
    x = x_ref[...]
    xb = x.astype(BF16)
    acc = None
    for c in range(D_FF // FF_CHUNK):
        sl = slice(c * FF_CHUNK, (c + 1) * FF_CHUNK)
        gate = _dot(xb, wg_ref[:, sl])
        up = _dot(xb, wu_ref[:, sl])
        act = (gate * jax.nn.sigmoid(gate)) * up
        part = _dot(act.astype(BF16), wd_ref[sl, :])
        acc = part if acc is None else acc + part
    o_ref[...] = _layer_norm(ALPHA * x + acc, g_ref[...], b_ref[...])


def _ffn_ln(x, wg, wu, wd, g, b, tm):
    rows = x.shape[0]
    row_spec = pl.BlockSpec((tm, D_MODEL), lambda i: (i, 0))
    return pl.pallas_call(
        _ffn_ln_kernel,
        grid=(rows // tm,),
        in_specs=[row_spec, _resident(wg.shape), _resident(wu.shape), _resident(wd.shape),
                  _resident(g.shape), _resident(b.shape)],
        out_specs=row_spec,
        out_shape=jax.ShapeDtypeStruct(x.shape, F32),
        compiler_params=_params("arbitrary"),
        name="ffn_ln",
    )(x, wg, wu, wd, g, b)


def _pool_finish(x, sums, cnts, w_ref, sc_ref, g_ref, b_ref):
    parts = []
    for gi in range(len(POOL_WINDOWS)):
        lo = gi * POOL_GROUP
        pooled = sums[gi] / cnts[gi] - x[:, lo:lo + POOL_GROUP]
        parts.append(_dot(pooled.astype(BF16), w_ref[gi]))
    mix = jnp.concatenate(parts, axis=-1) * sc_ref[...]
    return _layer_norm(ALPHA * x + mix, g_ref[...], b_ref[...])


def _pool_prompt_kernel(x_ref, w_ref, sc_ref, g_ref, b_ref, o_ref, ext_ref, *, tm, tiles_per_seq):
    halo = POOL_STATE + 1
    t = pl.program_id(0) % tiles_per_seq

    @pl.when(t == 0)
    def _():
        ext_ref[0:halo, :] = jnp.zeros((halo, D_MODEL), F32)

    @pl.when(t != 0)
    def _():
        ext_ref[0:halo, :] = ext_ref[tm:tm + halo, :]

    x = x_ref[...]
    ext_ref[halo:halo + tm, :] = x
    pos = t * tm + lax.broadcasted_iota(jnp.int32, (tm, 1), 0)
    sums, cnts = [], []
    for gi, win in enumerate(POOL_WINDOWS):
        lo = gi * POOL_GROUP
        s = x[:, lo:lo + POOL_GROUP]
        for k in range(1, win):
            s = s + ext_ref[halo - k:halo - k + tm, lo:lo + POOL_GROUP]
        sums.append(s)
        cnts.append(jnp.minimum(pos + 1, win).astype(F32))
    o_ref[...] = _pool_finish(x, sums, cnts, w_ref, sc_ref, g_ref, b_ref)


def _pool_prompt(x, w, sc, g, b, tm):
    rows = x.shape[0]
    row_spec = pl.BlockSpec((tm, D_MODEL), lambda i: (i, 0))
    return pl.pallas_call(
        functools.partial(_pool_prompt_kernel, tm=tm, tiles_per_seq=TP // tm),
        grid=(rows // tm,),
        in_specs=[row_spec, _resident(w.shape), _resident(sc.shape), _resident(g.shape), _resident(b.shape)],
        out_specs=row_spec,
        out_shape=jax.ShapeDtypeStruct(x.shape, F32),
        scratch_shapes=[pltpu.VMEM((tm + POOL_STATE + 1, D_MODEL), F32)],
        compiler_params=_params("arbitrary"),
        name="pool_prompt",
    )(x, w, sc, g, b)


def _pool_sample_kernel(ext_ref, w_ref, sc_ref, g_ref, b_ref, o_ref):
    bs = ext_ref.shape[0]
    rows = bs * DEC_SEQ
    x = ext_ref[:, POOL_STATE:, :].reshape(rows, D_MODEL)
    sums, cnts = [], []
    for gi, win in enumerate(POOL_WINDOWS):
        lo = gi * POOL_GROUP
        s = x[:, lo:lo + POOL_GROUP]
        for k in range(1, win):
            s = s + ext_ref[:, POOL_STATE - k:POOL_STATE - k + DEC_SEQ, lo:lo + POOL_GROUP].reshape(rows, POOL_GROUP)
        sums.append(s)
        cnts.append(float(win))
    o_ref[...] = _pool_finish(x, sums, cnts, w_ref, sc_ref, g_ref, b_ref)


def _pool_sample(ext, w, sc, g, b):
    n = ext.shape[0]
    return pl.pallas_call(
        _pool_sample_kernel,
        grid=(n // POOL_BS,),
        in_specs=[pl.BlockSpec((POOL_BS,) + ext.shape[1:], lambda i: (i, 0, 0)),
                  _resident(w.shape), _resident(sc.shape), _resident(g.shape), _resident(b.shape)],
        out_specs=pl.BlockSpec((POOL_BS * DEC_SEQ, D_MODEL), lambda i: (i, 0)),
        out_shape=jax.ShapeDtypeStruct((n * DEC_SEQ, D_MODEL), F32),
        compiler_params=_params("arbitrary"),
        name="pool_sample",
    )(ext, w, sc, g, b)


def _kv_proj_kernel(x_ref, w_ref, g_ref, cos_ref, sin_ref, rows_ref, kcat_ref):
    kv = _dot(x_ref[...].astype(BF16), w_ref[...])
    c_kv = _rms_norm(kv[:, :KV_LORA], g_ref[...])
    k_pe = (kv[:, KV_LORA:KV_LORA + ROPE_PAD] * cos_ref[...]
            + kv[:, KV_LORA + ROPE_PAD:] * sin_ref[...])
    rows_ref[:, :KV_LORA] = c_kv
    rows_ref[:, KV_LORA:] = k_pe[:, :QK_ROPE]
    kcat_ref[:, :KV_LORA] = c_kv.astype(BF16)
    kcat_ref[:, KV_LORA:] = k_pe.astype(BF16)


def _kv_proj(x, w, g, cos_t, sin_t, tm):
    rows = x.shape[0]
    tab_tiles = cos_t.shape[0] // tm
    tab_spec = pl.BlockSpec((tm, ROPE_PAD), lambda i: (i % tab_tiles, 0))
    return pl.pallas_call(
        _kv_proj_kernel,
        grid=(rows // tm,),
        in_specs=[pl.BlockSpec((tm, D_MODEL), lambda i: (i, 0)), _resident(w.shape), _resident(g.shape),
                  tab_spec, tab_spec],
        out_specs=[pl.BlockSpec((tm, KV_ROW), lambda i: (i, 0)), pl.BlockSpec((tm, QK_CAT), lambda i: (i, 0))],
        out_shape=[jax.ShapeDtypeStruct((rows, KV_ROW), F32), jax.ShapeDtypeStruct((rows, QK_CAT), BF16)],
        compiler_params=_params("arbitrary"),
        name="kv_proj",
    )(x, w, g, cos_t, sin_t)


def _q_proj_kernel(x_ref, wdq_ref, gq_ref, wuq_ref, wuk_ref, cos_ref, sin_ref, q_ref):
    hd = N_HEADS * QK_NOPE
    c_q = _rms_norm(_dot(x_ref[...].astype(BF16), wdq_ref[...]), gq_ref[...])
    q = _dot(c_q.astype(BF16), wuq_ref[...])
    cos_t = cos_ref[...]
    sin_t = sin_ref[...]
    for h in range(N_HEADS):
        nope = q[:, h * QK_NOPE:(h + 1) * QK_NOPE].astype(BF16)
        q_ref[0, h, :, :KV_LORA] = _dot(nope, wuk_ref[h]).astype(BF16)
        pe = (q[:, hd + h * ROPE_PAD:hd + (h + 1) * ROPE_PAD] * cos_t
              + q[:, 2 * hd + h * ROPE_PAD:2 * hd + (h + 1) * ROPE_PAD] * sin_t)
        q_ref[0, h, :, KV_LORA:] = pe.astype(BF16)


def _q_proj(x, wdq, gq, wuq, wuk, cos_t, sin_t, tm, n_seq):
    rows = x.shape[0]
    t = rows // n_seq
    tiles = t // tm
    tab_spec = pl.BlockSpec((tm, ROPE_PAD), lambda i: (i % tiles, 0))
    return pl.pallas_call(
        _q_proj_kernel,
        grid=(rows // tm,),
        in_specs=[pl.BlockSpec((tm, D_MODEL), lambda i: (i, 0)), _resident(wdq.shape), _resident(gq.shape),
                  _resident(wuq.shape), _resident(wuk.shape), tab_spec, tab_spec],
        out_specs=pl.BlockSpec((1, N_HEADS, tm, QK_CAT), lambda i: (i // tiles, 0, i % tiles, 0)),
        out_shape=jax.ShapeDtypeStruct((n_seq, N_HEADS, t, QK_CAT), BF16),
        compiler_params=_params("arbitrary"),
        name="q_proj",
    )(x, wdq, gq, wuq, wuk, cos_t, sin_t)


def _mla_out_ln_kernel(o_ref, x_ref, wuv_ref, wo_ref, g_ref, b_ref, y_ref):
    heads = [_dot(o_ref[0, h], wuv_ref[h]) for h in range(N_HEADS)]
    mix = _dot(jnp.concatenate(heads, axis=-1).astype(BF16), wo_ref[...])
    y_ref[...] = _layer_norm(ALPHA * x_ref[...] + mix, g_ref[...], b_ref[...])


def _mla_out_ln(o, x, wuv, wo, g, b, tm):
    rows = x.shape[0]
    tiles = o.shape[2] // tm
    row_spec = pl.BlockSpec((tm, D_MODEL), lambda i: (i, 0))
    return pl.pallas_call(
        _mla_out_ln_kernel,
        grid=(rows // tm,),
        in_specs=[pl.BlockSpec((1, N_HEADS, tm, KV_LORA), lambda i: (i // tiles, 0, i % tiles, 0)), row_spec,
                  _resident(wuv.shape), _resident(wo.shape), _resident(g.shape), _resident(b.shape)],
        out_specs=row_spec,
        out_shape=jax.ShapeDtypeStruct(x.shape, F32),
        compiler_params=_params("arbitrary"),
        name="mla_out_ln",
    )(o, x, wuv, wo, g, b)


def _softmax_step(s, v, m_ref, l_ref, acc_ref):
    m_prev = m_ref[...]
    m_new = jnp.maximum(m_prev, jnp.max(s, axis=1, keepdims=True))
    alpha = jnp.exp2((m_prev - m_new) * EXP2_SCALE)
    p = jnp.exp2((s - m_new) * EXP2_SCALE)
    l_ref[...] = alpha * l_ref[...] + jnp.sum(p, axis=1, keepdims=True)
    acc_ref[...] = alpha * acc_ref[...] + _dot(p.astype(BF16), v)
    m_ref[...] = m_new


def _softmax_init(m_ref, l_ref, acc_ref):
    m_ref[...] = jnp.full(m_ref.shape, NEG_INF, F32)
    l_ref[...] = jnp.zeros(l_ref.shape, F32)
    acc_ref[...] = jnp.zeros(acc_ref.shape, F32)


def _attn_prompt_kernel(q_ref, k_ref, o_ref, m_ref, l_ref, acc_ref):
    m_rows = N_HEADS * ATT_BQ
    i = pl.program_id(1)
    q0 = i * ATT_BQ
    _softmax_init(m_ref, l_ref, acc_ref)

    def step(start, size, masked):
        q = q_ref[0].reshape(m_rows, QK_CAT)
        k = k_ref[0, pl.ds(start, size), :]
        s = _dot_nt(q, k)
        if masked:
            col = lax.broadcasted_iota(jnp.int32, (m_rows, size), 1)
            row = lax.broadcasted_iota(jnp.int32, (m_rows, size), 0)
            s = jnp.where(col - (row & (ATT_BQ - 1)) <= q0 - start, s, NEG_INF)
        _softmax_step(s, k[:, :KV_LORA], m_ref, l_ref, acc_ref)

    n_before = q0 // ATT_BK

    def body(kb, carry):
        step(pl.multiple_of(kb * ATT_BK, ATT_BK), ATT_BK, False)
        return carry

    lax.fori_loop(0, n_before, body, 0)

    n_full = TP // ATT_BK
    tail = TP - n_full * ATT_BK

    @pl.when(n_before < n_full)
    def _():
        step(pl.multiple_of(n_before * ATT_BK, ATT_BK), ATT_BK, True)

    if tail:
        @pl.when(n_before == n_full)
        def _():
            step(n_full * ATT_BK, tail, True)

    o = acc_ref[...] / l_ref[...]
    o_ref[0] = o.astype(BF16).reshape(N_HEADS, ATT_BQ, KV_LORA)


def _attn_prompt(q, kcat):
    assert ATT_BQ & (ATT_BQ - 1) == 0 and ATT_BK % ATT_BQ == 0 and TP % ATT_BQ == 0
    assert (TP % ATT_BK) % ATT_BQ == 0
    n = q.shape[0]
    m_rows = N_HEADS * ATT_BQ
    return pl.pallas_call(
        _attn_prompt_kernel,
        grid=(n, TP // ATT_BQ),
        in_specs=[pl.BlockSpec((1, N_HEADS, ATT_BQ, QK_CAT), lambda b, i: (b, 0, i, 0)),
                  pl.BlockSpec((1, TP, QK_CAT), lambda b, i: (b, 0, 0))],
        out_specs=pl.BlockSpec((1, N_HEADS, ATT_BQ, KV_LORA), lambda b, i: (b, 0, i, 0)),
        out_shape=jax.ShapeDtypeStruct((n, N_HEADS, TP, KV_LORA), BF16),
        scratch_shapes=[pltpu.VMEM((m_rows, 1), F32), pltpu.VMEM((m_rows, 1), F32),
                        pltpu.VMEM((m_rows, KV_LORA), F32)],
        compiler_params=_params("arbitrary", "arbitrary"),
        name="attn_prompt",
    )(q, kcat)


def _attn_sample_kernel(pt_ref, q_ref, new_ref, cache_ref, o_ref, kbuf_ref, sem_ref, m_ref, l_ref, acc_ref,
                        *, chunks_per_seq, n_steps):
    g = pl.program_id(0)
    c = g % chunks_per_seq

    def page_copy(step, slot, p):
        page = pt_ref[step // chunks_per_seq, (step % chunks_per_seq) * SA_PAGES + p]
        return pltpu.make_async_copy(cache_ref.at[page], kbuf_ref.at[slot, pl.ds(p * PAGE_SIZE, PAGE_SIZE), :],
                                     sem_ref.at[slot])

    def start_fetch(step):
        for p in range(SA_PAGES):
            page_copy(step, step % 2, p).start()

    @pl.when(g == 0)
    def _():
        start_fetch(g)

    @pl.when(g + 1 < n_steps)
    def _():
        start_fetch(g + 1)

    @pl.when(c == 0)
    def _():
        _softmax_init(m_ref, l_ref, acc_ref)

    slot = g % 2
    for p in range(SA_PAGES):
        page_copy(g, slot, p).wait()

    q = q_ref[0]
    k = kbuf_ref[slot].astype(BF16)
    _softmax_step(_dot_nt(q, k), k[:, :KV_LORA], m_ref, l_ref, acc_ref)

    @pl.when(c == chunks_per_seq - 1)
    def _():
        m_rows = N_HEADS * DEC_SEQ
        pad = jnp.zeros((SA_NEW_PAD - DEC_SEQ, KV_ROW), F32)
        kn = jnp.concatenate([new_ref[0], pad], axis=0).astype(BF16)
        s = _dot_nt(q, kn)
        col = lax.broadcasted_iota(jnp.int32, (m_rows, SA_NEW_PAD), 1)
        row = lax.broadcasted_iota(jnp.int32, (m_rows, SA_NEW_PAD), 0)
        s = jnp.where(col <= (row & (DEC_SEQ - 1)), s, NEG_INF)
        _softmax_step(s, kn[:, :KV_LORA], m_ref, l_ref, acc_ref)
        o_ref[0] = (acc_ref[...] / l_ref[...]).astype(BF16)


def _attn_sample(q, new_rows, cache, page_table):
    assert DEC_SEQ & (DEC_SEQ - 1) == 0
    n_seq, n_pages = page_table.shape
    assert n_pages * PAGE_SIZE == PAST_LEN and n_pages % SA_PAGES == 0
    chunks = n_pages // SA_PAGES
    n_steps = n_seq * chunks
    m_rows = N_HEADS * DEC_SEQ
    grid_spec = pltpu.PrefetchScalarGridSpec(
        num_scalar_prefetch=1,
        grid=(n_steps,),
        in_specs=[pl.BlockSpec((1, m_rows, KV_ROW), lambda g, pt: (g // chunks, 0, 0)),
                  pl.BlockSpec((1, DEC_SEQ, KV_ROW), lambda g, pt: (g // chunks, 0, 0)),
                  pl.BlockSpec(memory_space=pl.ANY)],
        out_specs=pl.BlockSpec((1, m_rows, KV_LORA), lambda g, pt: (g // chunks, 0, 0)),
        scratch_shapes=[pltpu.VMEM((2, SA_KEYS, KV_ROW), F32), pltpu.SemaphoreType.DMA((2,)),
                        pltpu.VMEM((m_rows, 1), F32), pltpu.VMEM((m_rows, 1), F32),
                        pltpu.VMEM((m_rows, KV_LORA), F32)],
    )
    return pl.pallas_call(
        functools.partial(_attn_sample_kernel, chunks_per_seq=chunks, n_steps=n_steps),
        grid_spec=grid_spec,
        out_shape=jax.ShapeDtypeStruct((n_seq, m_rows, KV_LORA), BF16),
        compiler_params=_params("arbitrary"),
        name="attn_sample",
    )(page_table, q, new_rows, cache)


def _rope_tables(pos):
    freqs = ROPE_THETA ** (-jnp.arange(0, QK_ROPE, 2, dtype=F32) / QK_ROPE)
    ang = pos.astype(F32)[:, None] * freqs[None, :]
    cos, sin = jnp.cos(ang), jnp.sin(ang)
    zero = jnp.zeros((pos.shape[0], ROPE_PAD - QK_ROPE), F32)
    return (jnp.concatenate([cos, cos, zero], axis=-1), jnp.concatenate([-sin, sin, zero], axis=-1))


def _pad_rope_cols(w):
    half = QK_ROPE // 2
    zero = jnp.zeros(w.shape[:-1] + (ROPE_PAD - QK_ROPE,), w.dtype)
    swapped = jnp.concatenate([w[..., half:], w[..., :half]], axis=-1)
    return jnp.concatenate([w, zero], axis=-1), jnp.concatenate([swapped, zero], axis=-1)


def kernel(x_prompt, x_sample, state_pool_l0, cache_mla_l1, state_pool_l2, cache_mla_l3, page_table,
           meta_tokens, pool_w, pool_scale, mla_w_dq, mla_g_q, mla_w_uq, mla_w_dkv, mla_g_kv,
           mla_w_uk, mla_w_uv, mla_w_o, ffn_w_gate, ffn_w_up, ffn_w_down, ln_g, ln_b):
    b = x_prompt.shape[0]
    meta = jnp.broadcast_to(meta_tokens[None], (b, N_META, D_MODEL))
    pad = jnp.zeros((b, TP - T_REAL, D_MODEL), F32)
    hp = jnp.concatenate([meta, x_prompt, pad], axis=1).reshape(b * TP, D_MODEL)
    hs = x_sample.reshape(DEC_BATCH * DEC_SEQ, D_MODEL)

    cos_p, sin_p = _rope_tables(jnp.arange(TP))
    cos_s, sin_s = _rope_tables(PAST_LEN + jnp.arange(DEC_BATCH * DEC_SEQ) % DEC_SEQ)

    pool_states = (state_pool_l0, state_pool_l2)
    mla_caches = (cache_mla_l1, cache_mla_l3)
    row = lambda v: v.reshape(1, -1)
    new_state = []
    for i in range(DEPTH):
        j = i // 2
        g0, b0, g1, b1 = row(ln_g[i, 0]), row(ln_b[i, 0]), row(ln_g[i, 1]), row(ln_b[i, 1])
        if i % 2 == 0:
            ext_s = jnp.concatenate([pool_states[j], hs.reshape(DEC_BATCH, DEC_SEQ, D_MODEL)], axis=1)
            new_state += [hp.reshape(b, TP, D_MODEL)[:, T_REAL - POOL_STATE:T_REAL], ext_s[:, -POOL_STATE:]]
            w, sc = pool_w[j].astype(BF16), row(pool_scale[j])
            hp = _pool_prompt(hp, w, sc, g0, b0, TM_PROMPT)
            hs = _pool_sample(ext_s, w, sc, g0, b0)
        else:
            wdq = mla_w_dq[j].astype(BF16)
            gq, gkv = row(mla_g_q[j]), row(mla_g_kv[j])
            uq = mla_w_uq[j]
            pe, pe_sw = _pad_rope_cols(uq[:, :, QK_NOPE:])
            wuq = jnp.concatenate([uq[:, :, :QK_NOPE].reshape(Q_LORA, -1), pe.reshape(Q_LORA, -1),
                                   pe_sw.reshape(Q_LORA, -1)], axis=1).astype(BF16)
            kpe, kpe_sw = _pad_rope_cols(mla_w_dkv[j][:, KV_LORA:])
            wdkv = jnp.concatenate([mla_w_dkv[j][:, :KV_LORA], kpe, kpe_sw], axis=1).astype(BF16)
            wuk = jnp.transpose(mla_w_uk[j], (1, 2, 0)).astype(BF16)
            wuv = jnp.transpose(mla_w_uv[j], (1, 0, 2)).astype(BF16)
            wo = mla_w_o[j].astype(BF16)

            rows_p, kcat_p = _kv_proj(hp, wdkv, gkv, cos_p, sin_p, TM_PROMPT)
            q_p = _q_proj(hp, wdq, gq, wuq, wuk, cos_p, sin_p, TM_PROMPT, b)
            o_p = _attn_prompt(q_p, kcat_p.reshape(b, TP, QK_CAT))
            hp = _mla_out_ln(o_p, hp, wuv, wo, g0, b0, TM_PROMPT)

            rows_s, _ = _kv_proj(hs, wdkv, gkv, cos_s, sin_s, TM_SAMPLE)
            rows_s = rows_s.reshape(DEC_BATCH, DEC_SEQ, KV_ROW)
            q_s = _q_proj(hs, wdq, gq, wuq, wuk, cos_s, sin_s, TM_SAMPLE, 1)
            q_s = q_s.reshape(N_HEADS, DEC_BATCH, DEC_SEQ, QK_CAT).transpose(1, 0, 2, 3)
            q_s = q_s.reshape(DEC_BATCH, N_HEADS * DEC_SEQ, QK_CAT)[:, :, :KV_ROW]
            o_s = _attn_sample(q_s, rows_s, mla_caches[j], page_table)
            o_s = o_s.reshape(DEC_BATCH, N_HEADS, DEC_SEQ, KV_LORA).transpose(1, 0, 2, 3)
            o_s = o_s.reshape(1, N_HEADS, DEC_BATCH * DEC_SEQ, KV_LORA)
            hs = _mla_out_ln(o_s, hs, wuv, wo, g0, b0, TM_SAMPLE)
            new_state += [rows_p.reshape(b, TP, KV_ROW)[:, :T_REAL], rows_s]
        wg, wu, wd = ffn_w_gate[i].astype(BF16), ffn_w_up[i].astype(BF16), ffn_w_down[i].astype(BF16)
        hp = _ffn_ln(hp, wg, wu, wd, g1, b1, TM_PROMPT)
        hs = _ffn_ln(hs, wg, wu, wd, g1, b1, TM_SAMPLE)
    y_prompt = hp.reshape(b, TP, D_MODEL)[:, N_META:T_REAL]
    y_sample = hs.reshape(DEC_BATCH, DEC_SEQ, D_MODEL)
    return (y_prompt, y_sample, *new_state)
```

```python
import functools
import math

import jax
import jax.numpy as jnp
from jax import lax
from jax.experimental import pallas as pl
from jax.experimental.pallas import tpu as pltpu

F32 = jnp.float32
BF16 = jnp.bfloat16

D_MODEL = 1024
BATCH = 2
SEQ = 8192
DEPTH = 4
DEC_BATCH = 128
DEC_SEQ = 8
PAST_LEN = 8192
PAGE_SIZE = 128
N_META = 16
POOL_WINDOWS = (2, 4, 8, 16)
POOL_GROUP = D_MODEL // len(POOL_WINDOWS)
POOL_STATE = max(POOL_WINDOWS) - 1
N_HEADS = 8
QK_NOPE = 128
QK_ROPE = 64
V_HEAD = 128
Q_LORA = D_MODEL // 2
KV_LORA = D_MODEL // 4
KV_ROW = KV_LORA + QK_ROPE
ROPE_THETA = 10000.0
SOFTMAX_SCALE = (QK_NOPE + QK_ROPE) ** -0.5
D_FF = -(-(8 * D_MODEL) // (3 * 256)) * 256
ALPHA = (2 * DEPTH) ** 0.25
LN_EPS = 1e-5
RMS_EPS = 1e-6
NEG_INF = -1e30

LANES = 128
T_REAL = SEQ + N_META
TP = -(-T_REAL // LANES) * LANES
ROPE_PAD = LANES
QK_CAT = KV_LORA + ROPE_PAD
EXP2_SCALE = SOFTMAX_SCALE * math.log2(math.e)

TM_PROMPT = 832
TM_SAMPLE = 512
FF_CHUNK = 256
ATT_BQ = 128
ATT_BK = 512
TPK = -(-TP // ATT_BK) * ATT_BK
POOL_BS = 16
SA_NEW_PAD = 16
VMEM_LIMIT = 56 * 1024 * 1024


def _params(*sem):
    return pltpu.CompilerParams(dimension_semantics=sem, vmem_limit_bytes=VMEM_LIMIT)


def _resident(shape):
    nd = len(shape)
    return pl.BlockSpec(shape, lambda *_: (0,) * nd, pipeline_mode=pl.Buffered(1))


def _layer_norm(y, g, b):
    mu = jnp.mean(y, axis=-1, keepdims=True)
    d = y - mu
    var = jnp.mean(d * d, axis=-1, keepdims=True)
    return d * lax.rsqrt(var + LN_EPS) * g + b


def _rms_norm(x, g):
    return x * lax.rsqrt(jnp.mean(x * x, axis=-1, keepdims=True) + RMS_EPS) * g


def _dot(a, b):
    return jnp.dot(a, b, preferred_element_type=F32)


def _dot_nt(a, b):
    return lax.dot_general(a, b, (((1,), (1,)), ((), ())), preferred_element_type=F32)


def _ffn_ln_kernel(x_ref, wg_ref, wu_ref, wd_ref, g_ref, b_ref, o_ref):
    x = x_ref[...]
    xb = x.astype(BF16)
    acc = None
    for c in range(D_FF // FF_CHUNK):
        sl = slice(c * FF_CHUNK, (c + 1) * FF_CHUNK)
        gate = _dot(xb, wg_ref[:, sl])
        up = _dot(xb, wu_ref[:, sl])
        act = (gate * jax.nn.sigmoid(gate)) * up
        part = _dot(act.astype(BF16), wd_ref[sl, :])
        acc = part if acc is None else acc + part
    o_ref[...] = _layer_norm(ALPHA * x + acc, g_ref[...], b_ref[...])


def _ffn_ln(x, wg, wu, wd, g, b, tm):
    rows = x.shape[0]
    row_spec = pl.BlockSpec((tm, D_MODEL), lambda i: (i, 0))
    return pl.pallas_call(
        _ffn_ln_kernel,
        grid=(rows // tm,),
        in_specs=[row_spec, _resident(wg.shape), _resident(wu.shape), _resident(wd.shape),
                  _resident(g.shape), _resident(b.shape)],
        out_specs=row_spec,
        out_shape=jax.ShapeDtypeStruct(x.shape, F32),
        compiler_params=_params("arbitrary"),
        name="ffn_ln",
    )(x, wg, wu, wd, g, b)


def _pool_finish(x, sums, cnts, w_ref, sc_ref, g_ref, b_ref):
    parts = []
    for gi in range(len(POOL_WINDOWS)):
        lo = gi * POOL_GROUP
        pooled = sums[gi] / cnts[gi] - x[:, lo:lo + POOL_GROUP]
        parts.append(_dot(pooled.astype(BF16), w_ref[gi]))
    mix = jnp.concatenate(parts, axis=-1) * sc_ref[...]
    return _layer_norm(ALPHA * x + mix, g_ref[...], b_ref[...])


def _pool_prompt_kernel(x_ref, w_ref, sc_ref, g_ref, b_ref, o_ref, ext_ref, *, tm, tiles_per_seq):
    halo = POOL_STATE + 1
    t = pl.program_id(0) % tiles_per_seq

    @pl.when(t == 0)
    def _():
        ext_ref[0:halo, :] = jnp.zeros((halo, D_MODEL), F32)

    @pl.when(t != 0)
    def _():
        ext_ref[0:halo, :] = ext_ref[tm:tm + halo, :]

    x = x_ref[...]
    ext_ref[halo:halo + tm, :] = x
    pos = t * tm + lax.broadcasted_iota(jnp.int32, (tm, 1), 0)
    sums, cnts = [], []
    for gi, win in enumerate(POOL_WINDOWS):
        lo = gi * POOL_GROUP
        s = x[:, lo:lo + POOL_GROUP]
        for k in range(1, win):
            s = s + ext_ref[halo - k:halo - k + tm, lo:lo + POOL_GROUP]
        sums.append(s)
        cnts.append(jnp.minimum(pos + 1, win).astype(F32))
    o_ref[...] = _pool_finish(x, sums, cnts, w_ref, sc_ref, g_ref, b_ref)


def _pool_prompt(x, w, sc, g, b, tm):
    rows = x.shape[0]
    row_spec = pl.BlockSpec((tm, D_MODEL), lambda i: (i, 0))
    return pl.pallas_call(
        functools.partial(_pool_prompt_kernel, tm=tm, tiles_per_seq=TP // tm),
        grid=(rows // tm,),
        in_specs=[row_spec, _resident(w.shape), _resident(sc.shape), _resident(g.shape), _resident(b.shape)],
        out_specs=row_spec,
        out_shape=jax.ShapeDtypeStruct(x.shape, F32),
        scratch_shapes=[pltpu.VMEM((tm + POOL_STATE + 1, D_MODEL), F32)],
        compiler_params=_params("arbitrary"),
        name="pool_prompt",
    )(x, w, sc, g, b)


def _pool_sample_kernel(ext_ref, w_ref, sc_ref, g_ref, b_ref, o_ref):
    bs = ext_ref.shape[0]
    rows = bs * DEC_SEQ
    x = ext_ref[:, POOL_STATE:, :].reshape(rows, D_MODEL)
    sums, cnts = [], []
    for gi, win in enumerate(POOL_WINDOWS):
        lo = gi * POOL_GROUP
        s = x[:, lo:lo + POOL_GROUP]
        for k in range(1, win):
            s = s + ext_ref[:, POOL_STATE - k:POOL_STATE - k + DEC_SEQ, lo:lo + POOL_GROUP].reshape(rows, POOL_GROUP)
        sums.append(s)
        cnts.append(float(win))
    o_ref[...] = _pool_finish(x, sums, cnts, w_ref, sc_ref, g_ref, b_ref)


def _pool_sample(ext, w, sc, g, b):
    n = ext.shape[0]
    return pl.pallas_call(
        _pool_sample_kernel,
        grid=(n // POOL_BS,),
        in_specs=[pl.BlockSpec((POOL_BS,) + ext.shape[1:], lambda i: (i, 0, 0)),
                  _resident(w.shape), _resident(sc.shape), _resident(g.shape), _resident(b.shape)],
        out_specs=pl.BlockSpec((POOL_BS * DEC_SEQ, D_MODEL), lambda i: (i, 0)),
        out_shape=jax.ShapeDtypeStruct((n * DEC_SEQ, D_MODEL), F32),
        compiler_params=_params("arbitrary"),
        name="pool_sample",
    )(ext, w, sc, g, b)


def _kv_proj_kernel(x_ref, w_ref, g_ref, cos_ref, sin_ref, rows_ref, kcat_ref):
    kv = _dot(x_ref[...].astype(BF16), w_ref[...])
    c_kv = _rms_norm(kv[:, :KV_LORA], g_ref[...])
    k_pe = (kv[:, KV_LORA:KV_LORA + ROPE_PAD] * cos_ref[...]
            + kv[:, KV_LORA + ROPE_PAD:] * sin_ref[...])
    rows_ref[:, :KV_LORA] = c_kv
    rows_ref[:, KV_LORA:] = k_pe[:, :QK_ROPE]
    kcat_ref[:, :KV_LORA] = c_kv.astype(BF16)
    kcat_ref[:, KV_LORA:] = k_pe.astype(BF16)


def _kv_proj(x, w, g, cos_t, sin_t, tm):
    rows = x.shape[0]
    tab_tiles = cos_t.shape[0] // tm
    tab_spec = pl.BlockSpec((tm, ROPE_PAD), lambda i: (i % tab_tiles, 0))
    return pl.pallas_call(
        _kv_proj_kernel,
        grid=(rows // tm,),
        in_specs=[pl.BlockSpec((tm, D_MODEL), lambda i: (i, 0)), _resident(w.shape), _resident(g.shape),
                  tab_spec, tab_spec],
        out_specs=[pl.BlockSpec((tm, KV_ROW), lambda i: (i, 0)), pl.BlockSpec((tm, QK_CAT), lambda i: (i, 0))],
        out_shape=[jax.ShapeDtypeStruct((rows, KV_ROW), F32), jax.ShapeDtypeStruct((rows, QK_CAT), BF16)],
        compiler_params=_params("arbitrary"),
        name="kv_proj",
    )(x, w, g, cos_t, sin_t)


def _q_proj_kernel(x_ref, wdq_ref, gq_ref, wuq_ref, wuk_ref, cos_ref, sin_ref, q_ref):
    hd = N_HEADS * QK_NOPE
    c_q = _rms_norm(_dot(x_ref[...].astype(BF16), wdq_ref[...]), gq_ref[...])
    q = _dot(c_q.astype(BF16), wuq_ref[...])
    cos_t = cos_ref[...]
    sin_t = sin_ref[...]
    for h in range(N_HEADS):
        nope = q[:, h * QK_NOPE:(h + 1) * QK_NOPE].astype(BF16)
        q_ref[0, h, :, :KV_LORA] = _dot(nope, wuk_ref[h]).astype(BF16)
        pe = (q[:, hd + h * ROPE_PAD:hd + (h + 1) * ROPE_PAD] * cos_t
              + q[:, 2 * hd + h * ROPE_PAD:2 * hd + (h + 1) * ROPE_PAD] * sin_t)
        q_ref[0, h, :, KV_LORA:] = pe.astype(BF16)


def _q_proj(x, wdq, gq, wuq, wuk, cos_t, sin_t, tm, n_seq):
    rows = x.shape[0]
    t = rows // n_seq
    tiles = t // tm
    tab_spec = pl.BlockSpec((tm, ROPE_PAD), lambda i: (i % tiles, 0))
    return pl.pallas_call(
        _q_proj_kernel,
        grid=(rows // tm,),
        in_specs=[pl.BlockSpec((tm, D_MODEL), lambda i: (i, 0)), _resident(wdq.shape), _resident(gq.shape),
                  _resident(wuq.shape), _resident(wuk.shape), tab_spec, tab_spec],
        out_specs=pl.BlockSpec((1, N_HEADS, tm, QK_CAT), lambda i: (i // tiles, 0, i % tiles, 0)),
        out_shape=jax.ShapeDtypeStruct((n_seq, N_HEADS, t, QK_CAT), BF16),
        compiler_params=_params("arbitrary"),
        name="q_proj",
    )(x, wdq, gq, wuq, wuk, cos_t, sin_t)


def _mla_out_ln_kernel(o_ref, x_ref, wuv_ref, wo_ref, g_ref, b_ref, y_ref):
    heads = [_dot(o_ref[0, h], wuv_ref[h]) for h in range(N_HEADS)]
    mix = _dot(jnp.concatenate(heads, axis=-1).astype(BF16), wo_ref[...])
    y_ref[...] = _layer_norm(ALPHA * x_ref[...] + mix, g_ref[...], b_ref[...])


def _mla_out_ln(o, x, wuv, wo, g, b, tm):
    rows = x.shape[0]
    tiles = o.shape[2] // tm
    row_spec = pl.BlockSpec((tm, D_MODEL), lambda i: (i, 0))
    return pl.pallas_call(
        _mla_out_ln_kernel,
        grid=(rows // tm,),
        in_specs=[pl.BlockSpec((1, N_HEADS, tm, KV_LORA), lambda i: (i // tiles, 0, i % tiles, 0)), row_spec,
                  _resident(wuv.shape), _resident(wo.shape), _resident(g.shape), _resident(b.shape)],
        out_specs=row_spec,
        out_shape=jax.ShapeDtypeStruct(x.shape, F32),
        compiler_params=_params("arbitrary"),
        name="mla_out_ln",
    )(o, x, wuv, wo, g, b)


def _attn_prompt_kernel(q_ref, k_ref, o_ref, m_ref, l_ref, acc_ref, s0_ref, s1_ref):
    m_rows = N_HEADS * ATT_BQ
    q0 = pl.program_id(1) * ATT_BQ
    s_refs = (s0_ref, s1_ref)
    m_ref[...] = jnp.full(m_ref.shape, NEG_INF, F32)
    l_ref[...] = jnp.zeros(l_ref.shape, F32)
    acc_ref[...] = jnp.zeros(acc_ref.shape, F32)

    def scores(j, slot):
        k = k_ref[0, pl.ds(pl.multiple_of(j * ATT_BK, ATT_BK), ATT_BK), :]
        s_refs[slot][...] = _dot_nt(q_ref[0].reshape(m_rows, QK_CAT), k)

    def update(j, slot, masked):
        start = pl.multiple_of(j * ATT_BK, ATT_BK)
        s = s_refs[slot][...]
        if masked:
            col = lax.broadcasted_iota(jnp.int32, (m_rows, ATT_BK), 1)
            row = lax.broadcasted_iota(jnp.int32, (m_rows, ATT_BK), 0)
            s = jnp.where(col - (row & (ATT_BQ - 1)) <= q0 - start, s, NEG_INF)
        m_prev = m_ref[...]
        m_new = jnp.maximum(m_prev, jnp.max(s, axis=1, keepdims=True))
        alpha = jnp.exp2((m_prev - m_new) * EXP2_SCALE)
        p = jnp.exp2((s - jnp.tile(m_new, (1, ATT_BK // LANES))) * EXP2_SCALE)
        l_ref[...] = alpha * l_ref[...] + jnp.sum(p, axis=1, keepdims=True)
        pv = _dot(p.astype(BF16), k_ref[0, pl.ds(start, ATT_BK), :KV_LORA])
        acc_ref[...] = jnp.tile(alpha, (1, KV_LORA // LANES)) * acc_ref[...] + pv
        m_ref[...] = m_new

    n_before = q0 // ATT_BK
    scores(0, 0)

    def body(jj, carry):
        j = 2 * jj
        scores(j + 1, 1)
        update(j, 0, False)
        scores(j + 2, 0)
        update(j + 1, 1, False)
        return carry

    lax.fori_loop(0, n_before // 2, body, 0)
    j0 = (n_before // 2) * 2

    @pl.when(n_before == j0)
    def _():
        update(j0, 0, True)

    @pl.when(n_before != j0)
    def _():
        scores(j0 + 1, 1)
        update(j0, 0, False)
        update(j0 + 1, 1, True)

    o = acc_ref[...] / jnp.tile(l_ref[...], (1, KV_LORA // LANES))
    o_ref[0] = o.astype(BF16).reshape(N_HEADS, ATT_BQ, KV_LORA)


def _attn_prompt(q, kcat):
    assert ATT_BQ & (ATT_BQ - 1) == 0 and ATT_BK % ATT_BQ == 0 and TP % ATT_BQ == 0
    assert kcat.shape[1] == TPK
    n = q.shape[0]
    m_rows = N_HEADS * ATT_BQ
    return pl.pallas_call(
        _attn_prompt_kernel,
        grid=(n, TP // ATT_BQ),
        in_specs=[pl.BlockSpec((1, N_HEADS, ATT_BQ, QK_CAT), lambda b, i: (b, 0, i, 0)),
                  pl.BlockSpec((1, TPK, QK_CAT), lambda b, i: (b, 0, 0))],
        out_specs=pl.BlockSpec((1, N_HEADS, ATT_BQ, KV_LORA), lambda b, i: (b, 0, i, 0)),
        out_shape=jax.ShapeDtypeStruct((n, N_HEADS, TP, KV_LORA), BF16),
        scratch_shapes=[pltpu.VMEM((m_rows, LANES), F32), pltpu.VMEM((m_rows, LANES), F32),
                        pltpu.VMEM((m_rows, KV_LORA), F32),
                        pltpu.VMEM((m_rows, ATT_BK), F32), pltpu.VMEM((m_rows, ATT_BK), F32)],
        compiler_params=_params("arbitrary", "arbitrary"),
        name="attn_prompt",
    )(q, kcat)


def _attn_sample_kernel(pt_ref, q_ref, new_ref, cache_ref, o_ref, kbuf_ref, sem_ref, *, n_pages, n_seq):
    g = pl.program_id(0)

    def page_copy(seq, p):
        slot = seq % 2
        return pltpu.make_async_copy(cache_ref.at[pt_ref[seq, p]],
                                     kbuf_ref.at[slot, :, pl.ds(p * PAGE_SIZE, PAGE_SIZE)], sem_ref.at[slot])

    def start_fetch(seq):
        for p in range(n_pages):
            page_copy(seq, p).start()

    @pl.when(g == 0)
    def _():
        start_fetch(g)

    @pl.when(g + 1 < n_seq)
    def _():
        start_fetch(g + 1)

    for p in range(n_pages):
        page_copy(g, p).wait()

    m_rows = N_HEADS * DEC_SEQ
    q = q_ref[0]
    kt = kbuf_ref[g % 2].astype(BF16)
    s = _dot(q, kt)
    pad = jnp.zeros((SA_NEW_PAD - DEC_SEQ, KV_ROW), F32)
    kn = jnp.concatenate([new_ref[0], pad], axis=0).astype(BF16)
    s_new = _dot_nt(q, kn)
    col = lax.broadcasted_iota(jnp.int32, (m_rows, SA_NEW_PAD), 1)
    row = lax.broadcasted_iota(jnp.int32, (m_rows, SA_NEW_PAD), 0)
    s_new = jnp.where(col <= (row & (DEC_SEQ - 1)), s_new, NEG_INF)
    m = jnp.maximum(jnp.max(s, axis=1, keepdims=True), jnp.max(s_new, axis=1, keepdims=True))
    p = jnp.exp2((s - m) * EXP2_SCALE)
    p_new = jnp.exp2((s_new - m) * EXP2_SCALE)
    l = jnp.sum(p, axis=1, keepdims=True) + jnp.sum(p_new, axis=1, keepdims=True)
    o = _dot_nt(p.astype(BF16), kt[:KV_LORA, :]) + _dot(p_new.astype(BF16), kn[:, :KV_LORA])
    o_ref[0] = (o / l).astype(BF16)


def _attn_sample(q, new_rows, cache_t, page_table):
    assert DEC_SEQ & (DEC_SEQ - 1) == 0
    n_seq, n_pages = page_table.shape
    assert n_pages * PAGE_SIZE == PAST_LEN
    m_rows = N_HEADS * DEC_SEQ
    grid_spec = pltpu.PrefetchScalarGridSpec(
        num_scalar_prefetch=1,
        grid=(n_seq,),
        in_specs=[pl.BlockSpec((1, m_rows, KV_ROW), lambda g, pt: (g, 0, 0)),
                  pl.BlockSpec((1, DEC_SEQ, KV_ROW), lambda g, pt: (g, 0, 0)),
                  pl.BlockSpec(memory_space=pl.ANY)],
        out_specs=pl.BlockSpec((1, m_rows, KV_LORA), lambda g, pt: (g, 0, 0)),
        scratch_shapes=[pltpu.VMEM((2, KV_ROW, PAST_LEN), F32), pltpu.SemaphoreType.DMA((2,))],
    )
    return pl.pallas_call(
        functools.partial(_attn_sample_kernel, n_pages=n_pages, n_seq=n_seq),
        grid_spec=grid_spec,
        out_shape=jax.ShapeDtypeStruct((n_seq, m_rows, KV_LORA), BF16),
        compiler_params=_params("arbitrary"),
        name="attn_sample",
    )(page_table, q, new_rows, cache_t)


def _rope_tables(pos):
    freqs = ROPE_THETA ** (-jnp.arange(0, QK_ROPE, 2, dtype=F32) / QK_ROPE)
    ang = pos.astype(F32)[:, None] * freqs[None, :]
    cos, sin = jnp.cos(ang), jnp.sin(ang)
    zero = jnp.zeros((pos.shape[0], ROPE_PAD - QK_ROPE), F32)
    return (jnp.concatenate([cos, cos, zero], axis=-1), jnp.concatenate([-sin, sin, zero], axis=-1))


def _pad_rope_cols(w):
    half = QK_ROPE // 2
    zero = jnp.zeros(w.shape[:-1] + (ROPE_PAD - QK_ROPE,), w.dtype)
    swapped = jnp.concatenate([w[..., half:], w[..., :half]], axis=-1)
    return jnp.concatenate([w, zero], axis=-1), jnp.concatenate([swapped, zero], axis=-1)


def kernel(x_prompt, x_sample, state_pool_l0, cache_mla_l1, state_pool_l2, cache_mla_l3, page_table,
           meta_tokens, pool_w, pool_scale, mla_w_dq, mla_g_q, mla_w_uq, mla_w_dkv, mla_g_kv,
           mla_w_uk, mla_w_uv, mla_w_o, ffn_w_gate, ffn_w_up, ffn_w_down, ln_g, ln_b):
    b = x_prompt.shape[0]
    meta = jnp.broadcast_to(meta_tokens[None], (b, N_META, D_MODEL))
    pad = jnp.zeros((b, TP - T_REAL, D_MODEL), F32)
    hp = jnp.concatenate([meta, x_prompt, pad], axis=1).reshape(b * TP, D_MODEL)
    hs = x_sample.reshape(DEC_BATCH * DEC_SEQ, D_MODEL)

    cos_p, sin_p = _rope_tables(jnp.arange(TP))
    cos_s, sin_s = _rope_tables(PAST_LEN + jnp.arange(DEC_BATCH * DEC_SEQ) % DEC_SEQ)

    pool_states = (state_pool_l0, state_pool_l2)
    mla_caches = (jnp.transpose(cache_mla_l1, (0, 2, 1)), jnp.transpose(cache_mla_l3, (0, 2, 1)))
    row = lambda v: v.reshape(1, -1)
    new_state = []
    for i in range(DEPTH):
        j = i // 2
        g0, b0, g1, b1 = row(ln_g[i, 0]), row(ln_b[i, 0]), row(ln_g[i, 1]), row(ln_b[i, 1])
        if i % 2 == 0:
            ext_s = jnp.concatenate([pool_states[j], hs.reshape(DEC_BATCH, DEC_SEQ, D_MODEL)], axis=1)
            new_state += [hp.reshape(b, TP, D_MODEL)[:, T_REAL - POOL_STATE:T_REAL], ext_s[:, -POOL_STATE:]]
            w, sc = pool_w[j].astype(BF16), row(pool_scale[j])
            hp = _pool_prompt(hp, w, sc, g0, b0, TM_PROMPT)
            hs = _pool_sample(ext_s, w, sc, g0, b0)
        else:
            wdq = mla_w_dq[j].astype(BF16)
            gq, gkv = row(mla_g_q[j]), row(mla_g_kv[j])
            uq = mla_w_uq[j]
            pe, pe_sw = _pad_rope_cols(uq[:, :, QK_NOPE:])
            wuq = jnp.concatenate([uq[:, :, :QK_NOPE].reshape(Q_LORA, -1), pe.reshape(Q_LORA, -1),
                                   pe_sw.reshape(Q_LORA, -1)], axis=1).astype(BF16)
            kpe, kpe_sw = _pad_rope_cols(mla_w_dkv[j][:, KV_LORA:])
            wdkv = jnp.concatenate([mla_w_dkv[j][:, :KV_LORA], kpe, kpe_sw], axis=1).astype(BF16)
            wuk = jnp.transpose(mla_w_uk[j], (1, 2, 0)).astype(BF16)
            wuv = jnp.transpose(mla_w_uv[j], (1, 0, 2)).astype(BF16)
            wo = mla_w_o[j].astype(BF16)

            rows_p, kcat_p = _kv_proj(hp, wdkv, gkv, cos_p, sin_p, TM_PROMPT)
            q_p = _q_proj(hp, wdq, gq, wuq, wuk, cos_p, sin_p, TM_PROMPT, b)
            kcat_p = jnp.pad(kcat_p.reshape(b, TP, QK_CAT), ((0, 0), (0, TPK - TP), (0, 0)))
            o_p = _attn_prompt(q_p, kcat_p)
            hp = _mla_out_ln(o_p, hp, wuv, wo, g0, b0, TM_PROMPT)

            rows_s, _ = _kv_proj(hs, wdkv, gkv, cos_s, sin_s, TM_SAMPLE)
            rows_s = rows_s.reshape(DEC_BATCH, DEC_SEQ, KV_ROW)
            q_s = _q_proj(hs, wdq, gq, wuq, wuk, cos_s, sin_s, TM_SAMPLE, 1)
            q_s = q_s.reshape(N_HEADS, DEC_BATCH, DEC_SEQ, QK_CAT).transpose(1, 0, 2, 3)
            q_s = q_s.reshape(DEC_BATCH, N_HEADS * DEC_SEQ, QK_CAT)[:, :, :KV_ROW]
            o_s = _attn_sample(q_s, rows_s, mla_caches[j], page_table)
            o_s = o_s.reshape(DEC_BATCH, N_HEADS, DEC_SEQ, KV_LORA).transpose(1, 0, 2, 3)
            o_s = o_s.reshape(1, N_HEADS, DEC_BATCH * DEC_SEQ, KV_LORA)
            hs = _mla_out_ln(o_s, hs, wuv, wo, g0, b0, TM_SAMPLE)
            new_state += [rows_p.reshape(b, TP, KV_ROW)[:, :T_REAL], rows_s]
        wg, wu, wd = ffn_w_gate[i].astype(BF16), ffn_w_up[i].astype(BF16), ffn_w_down[i].astype(BF16)
        hp = _ffn_ln(hp, wg, wu, wd, g1, b1, TM_PROMPT)
        hs = _ffn_ln(hs, wg, wu, wd, g1, b1, TM_SAMPLE)
    y_prompt = hp.reshape(b, TP, D_MODEL)[:, N_META:T_REAL]
    y_sample = hs.reshape(DEC_BATCH, DEC_SEQ, D_MODEL)
    return (y_prompt, y_sample, *new_state)
```
